```python
import math
import jax
import jax.numpy as jnp
from jax import lax
import numpy as np

D_MODEL = 1024
BATCH = 1
SEQ = 16384
DEPTH = 4

GRID_W = 64
CTX_LEN = 256
N_MIXERS = 4
HEAD_DIM = 64
ROPE_THETA = 10000.0
Q_BLOCK = 128
NORM_EPS = 1e-6
NEG_INF = -1e30

DA_HEADS = D_MODEL // (2 * HEAD_DIM)
DA_V_DIM = 2 * HEAD_DIM
SWA_HEADS = D_MODEL // HEAD_DIM
SWA_KV_HEADS = 4
SWA_WINDOW = 128
GA_HEADS = D_MODEL // HEAD_DIM
GA_KV_HEADS = 4
NA_HEADS = D_MODEL // HEAD_DIM
NA_ROWS = 8
NA_COLS = 16
N_GROUPS = 4
EXPERTS_PER_GROUP = 8
N_EXPERTS = N_GROUPS * EXPERTS_PER_GROUP
TOP_K = 2
D_EXPERT = D_MODEL // 2
MOE_BLOCK = 128

kernel_name = 'hybrid_diff_swa_axial_natten_hmoe_dit'


def _rmsnorm(x, g):
    xf = x.astype(jnp.float32)
    y = xf * lax.rsqrt(jnp.mean(xf * xf, axis=-1, keepdims=True) + NORM_EPS)
    return (y * g.astype(jnp.float32)).astype(x.dtype)


def _modulate(h, shift, scale):
    return h * (1 + scale) + shift


def _axial_rope_tables(n):
    t = jnp.arange(n, dtype=jnp.int32)
    pos = jnp.stack([t // GRID_W, t % GRID_W], axis=-1).astype(jnp.float32)
    quarter = HEAD_DIM // 4
    inv_freq = 1.0 / (ROPE_THETA ** (jnp.arange(quarter, dtype=jnp.float32) / quarter))
    ang = pos[:, :, None] * inv_freq
    return jnp.cos(ang), jnp.sin(ang)


def _rope(x, cos, sin):
    b, n, hh, d = x.shape
    xr = x.astype(jnp.float32).reshape(b, n, hh, 2, 2, d // 4)
    x1, x2 = xr[..., 0, :], xr[..., 1, :]
    cs, sn = cos[None, :, None], sin[None, :, None]
    out = jnp.stack([x1 * cs - x2 * sn, x2 * cs + x1 * sn], axis=-2)
    return out.reshape(b, n, hh, d).astype(x.dtype)


def _project(t, w, splits):
    b, n, _ = t.shape
    y = t @ w
    outs, off = [], 0
    for nh, dh in splits:
        outs.append(y[..., off:off + nh * dh].reshape(b, n, nh, dh))
        off += nh * dh
    return outs


def _sweep_queries(fn, q):
    b, s = q.shape[:2]
    nb = s // Q_BLOCK
    qb = jnp.swapaxes(q.reshape((b, nb, Q_BLOCK) + q.shape[2:]), 0, 1)
    ob = lax.map(fn, qb)
    return jnp.swapaxes(ob, 0, 1).reshape((b, s) + ob.shape[3:])


def _gqa_attend(q, k, v, sink=None, mask=None):
    b, nq, hq, d = q.shape
    hkv = k.shape[2]
    grp = hq // hkv
    qg = q.reshape(b, nq, hkv, grp, d)
    s = jnp.einsum('bqhgd,bkhd->bhgqk', qg, k).astype(jnp.float32) / math.sqrt(d)
    if mask is not None:
        s = jnp.where(mask, s, NEG_INF)
    if sink is None:
        p = jax.nn.softmax(s, axis=-1)
    else:
        sk = jnp.broadcast_to(sink.astype(jnp.float32).reshape(1, hkv, grp, 1, 1), s.shape[:-1] + (1,))
        p = jax.nn.softmax(jnp.concatenate([s, sk], axis=-1), axis=-1)[..., :-1]
    o = jnp.einsum('bhgqk,bkhd->bqhgd', p.astype(v.dtype), v)
    return o.reshape(b, nq, hq, v.shape[-1])


def _diff_attend(q, k, v, lam):
    s = jnp.einsum('bqhmd,bkhmd->bhmqk', q, k).astype(jnp.float32) / math.sqrt(q.shape[-1])
    p = jax.nn.softmax(s, axis=-1)
    a = p[:, :, 0] - lam * p[:, :, 1]
    return jnp.einsum('bhqk,bkhd->bqhd', a.astype(v.dtype), v)


def _diff_mixer(h, hc, w_qkv, w_o, lam_vecs, subln_g, lam_init, cos, sin, need_ctx):
    b, s, _ = h.shape
    lv = lam_vecs.astype(jnp.float32)
    lam = jnp.exp(jnp.sum(lv[0] * lv[1])) - jnp.exp(jnp.sum(lv[2] * lv[3])) + lam_init
    splits = [(2 * DA_HEADS, HEAD_DIM), (2 * DA_HEADS, HEAD_DIM), (DA_HEADS, DA_V_DIM)]
    q, k, v = _project(h, w_qkv, splits)
    qc, kc, vc = _project(hc, w_qkv, splits)
    q = _rope(q, cos, sin).reshape(b, s, DA_HEADS, 2, HEAD_DIM)
    k = _rope(k, cos, sin).reshape(b, s, DA_HEADS, 2, HEAD_DIM)
    kc = kc.reshape(b, -1, DA_HEADS, 2, HEAD_DIM)
    k_all = jnp.concatenate([kc, k], axis=1)
    v_all = jnp.concatenate([vc, v], axis=1)

    def out_proj(o):
        o = _rmsnorm(o, subln_g) * (1.0 - lam_init)
        return o.reshape(o.shape[0], o.shape[1], -1) @ w_o

    y = out_proj(_sweep_queries(lambda qb: _diff_attend(qb, k_all, v_all, lam), q))
    if not need_ctx:
        return y, None
    qc = qc.reshape(b, -1, DA_HEADS, 2, HEAD_DIM)
    return y, out_proj(_diff_attend(qc, kc, vc, lam))


def _swa_mixer(h, hc, w_qkv, w_o, sink, cos, sin, need_ctx):
    b, s, _ = h.shape
    n_ctx = hc.shape[1]
    nb = s // Q_BLOCK
    splits = [(SWA_HEADS, HEAD_DIM), (SWA_KV_HEADS, HEAD_DIM), (SWA_KV_HEADS, HEAD_DIM)]
    q, k, v = _project(h, w_qkv, splits)
    qc, kc, vc = _project(hc, w_qkv, splits)
    q, k = _rope(q, cos, sin), _rope(k, cos, sin)

    def band(t):
        tb = t.reshape((b, nb, Q_BLOCK) + t.shape[2:])
        tp = jnp.pad(tb, [(0, 0), (1, 1)] + [(0, 0)] * (tb.ndim - 2))
        tband = jnp.concatenate([tp[:, :-2], tp[:, 1:-1], tp[:, 2:]], axis=2)
        return jnp.swapaxes(tband, 0, 1)

    kb, vb = band(k), band(v)
    qb = jnp.swapaxes(q.reshape(b, nb, Q_BLOCK, SWA_HEADS, HEAD_DIM), 0, 1)
    qi = jnp.arange(Q_BLOCK)[:, None]
    kj = jnp.arange(3 * Q_BLOCK)[None, :]
    in_window = jnp.abs(kj - Q_BLOCK - qi) <= SWA_WINDOW
    ctx_ok = jnp.ones((Q_BLOCK, n_ctx), dtype=bool)

    def block(args):
        q_blk, k_blk, v_blk, bi = args
        kpos = bi * Q_BLOCK - Q_BLOCK + kj
        valid = in_window & (kpos >= 0) & (kpos < s)
        mask = jnp.concatenate([ctx_ok, valid], axis=-1)
        k_all = jnp.concatenate([kc, k_blk], axis=1)
        v_all = jnp.concatenate([vc, v_blk], axis=1)
        return _gqa_attend(q_blk, k_all, v_all, sink, mask)

    o = lax.map(block, (qb, kb, vb, jnp.arange(nb)))
    y = jnp.swapaxes(o, 0, 1).reshape(b, s, -1) @ w_o
    if not need_ctx:
        return y, None
    return y, _gqa_attend(qc, kc, vc, sink).reshape(b, n_ctx, -1) @ w_o


def _axial_gqa_mixer(h, hc, w_qkv, w_o, qk_g, cos, sin, need_ctx):
    b, s, _ = h.shape
    n_ctx = hc.shape[1]
    splits = [(GA_HEADS, HEAD_DIM), (GA_KV_HEADS, HEAD_DIM), (GA_KV_HEADS, HEAD_DIM)]
    q, k, v = _project(h, w_qkv, splits)
    qc, kc, vc = _project(hc, w_qkv, splits)
    q, k = _rmsnorm(q, qk_g[0]), _rmsnorm(k, qk_g[1])
    qc, kc = _rmsnorm(qc, qk_g[0]), _rmsnorm(kc, qk_g[1])
    q, k = _rope(q, cos, sin), _rope(k, cos, sin)
    k_all = jnp.concatenate([kc, k], axis=1)
    v_all = jnp.concatenate([vc, v], axis=1)
    o = _sweep_queries(lambda qb: _gqa_attend(qb, k_all, v_all), q)
    y = o.reshape(b, s, -1) @ w_o
    if not need_ctx:
        return y, None
    return y, _gqa_attend(qc, kc, vc).reshape(b, n_ctx, -1) @ w_o


def _na_mixer(h, hc, w_qkv, w_o, rpb, need_ctx):
    b, s, _ = h.shape
    n_ctx = hc.shape[1]
    rows = s // GRID_W
    kh = min(NA_ROWS, rows)
    splits = [(NA_HEADS, HEAD_DIM)] * 3
    q, k, v = _project(h, w_qkv, splits)
    qc, kc, vc = _project(hc, w_qkv, splits)
    q_g = q.reshape(b, rows, GRID_W, NA_HEADS, HEAD_DIM)
    k_g = k.reshape(b, rows, GRID_W, NA_HEADS, HEAD_DIM)
    v_g = v.reshape(b, rows, GRID_W, NA_HEADS, HEAD_DIM)
    cols = np.arange(GRID_W)
    col_idx = np.clip(cols - NA_COLS // 2, 0, GRID_W - NA_COLS)[:, None] + np.arange(NA_COLS)[None, :]
    dc_idx = col_idx - cols[:, None] + (NA_COLS - 1)
    bias_c = rpb.astype(jnp.float32)[:, :, dc_idx]
    scale = 1.0 / math.sqrt(HEAD_DIM)
    n_nb = kh * NA_COLS

    def row_step(args):
        q_r, r = args
        r0 = jnp.clip(r - kh // 2, 0, rows - kh)
        k_r = lax.dynamic_slice_in_dim(k_g, r0, kh, axis=1)[:, :, col_idx]
        v_r = lax.dynamic_slice_in_dim(v_g, r0, kh, axis=1)[:, :, col_idx]
        dr_idx = r0 + jnp.arange(kh) - r + (NA_ROWS - 1)
        bias = jnp.take(bias_c, dr_idx, axis=1).transpose(0, 2, 1, 3)
        s_nb = jnp.einsum('bchd,bkcwhd->bhckw', q_r, k_r).astype(jnp.float32) * scale + bias[None]
        s_cx = jnp.einsum('bchd,bnhd->bhcn', q_r, kc).astype(jnp.float32) * scale
        logits = jnp.concatenate([s_nb.reshape(b, NA_HEADS, GRID_W, n_nb), s_cx], axis=-1)
        p = jax.nn.softmax(logits, axis=-1).astype(v.dtype)
        p_nb = p[..., :n_nb].reshape(b, NA_HEADS, GRID_W, kh, NA_COLS)
        return (jnp.einsum('bhckw,bkcwhd->bchd', p_nb, v_r)
                + jnp.einsum('bhcn,bnhd->bchd', p[..., n_nb:], vc))

    o = lax.map(row_step, (jnp.swapaxes(q_g, 0, 1), jnp.arange(rows)))
    y = jnp.swapaxes(o, 0, 1).reshape(b, s, -1) @ w_o
    if not need_ctx:
        return y, None
    return y, _gqa_attend(qc, kc, vc).reshape(b, n_ctx, -1) @ w_o


def _hier_moe(t, wr_g, br_g, wr_e, br_e, w_gate, w_up, w_down):
    n_tok, d = t.shape
    lg = (t @ wr_g).astype(jnp.float32) + br_g.astype(jnp.float32)
    pg = jax.nn.softmax(lg, axis=-1)
    g = jnp.argmax(lg, axis=-1)
    gate_g = jnp.take_along_axis(pg, g[:, None], axis=1)
    le = ((t @ wr_e).astype(jnp.float32) + br_e.astype(jnp.float32)).reshape(n_tok, N_GROUPS, EXPERTS_PER_GROUP)
    le_g = jnp.take_along_axis(le, g[:, None, None], axis=1)[:, 0]
    top_v, top_i = lax.top_k(le_g, TOP_K)
    weights = (gate_g * jax.nn.softmax(top_v, axis=-1)).reshape(-1)
    expert = (g[:, None] * EXPERTS_PER_GROUP + top_i).reshape(-1)
    tok = jnp.repeat(jnp.arange(n_tok, dtype=jnp.int32), TOP_K)
    n_assign = n_tok * TOP_K
    order = jnp.argsort(expert, stable=True)
    e_sorted = expert[order]
    counts = jax.ops.segment_sum(jnp.ones_like(expert), expert, num_segments=N_EXPERTS)
    padded = (counts + MOE_BLOCK - 1) // MOE_BLOCK * MOE_BLOCK
    p_end = jnp.cumsum(padded)
    p_start = p_end - padded
    start = jnp.cumsum(counts) - counts
    dest = p_start[e_sorted] + jnp.arange(n_assign) - start[e_sorted]
    cap = -(-n_assign // MOE_BLOCK) * MOE_BLOCK + N_EXPERTS * MOE_BLOCK
    n_blk = cap // MOE_BLOCK
    buf_tok = jnp.full((cap,), n_tok, jnp.int32).at[dest].set(tok[order])
    buf_w = jnp.zeros((cap,), jnp.float32).at[dest].set(weights[order])
    blk_e = jnp.clip(jnp.searchsorted(p_end, jnp.arange(n_blk) * MOE_BLOCK, side='right'), 0, N_EXPERTS - 1)
    t_pad = jnp.concatenate([t, jnp.zeros((1, d), t.dtype)], axis=0)
    xb = t_pad[buf_tok].reshape(n_blk, MOE_BLOCK, d)

    def expert_block(args):
        x_blk, e = args
        hidden = jax.nn.silu(x_blk @ w_gate[e]) * (x_blk @ w_up[e])
        return hidden @ w_down[e]

    yb = lax.map(expert_block, (xb, blk_e)).reshape(cap, d)
    out = jnp.zeros((n_tok + 1, d), t.dtype).at[buf_tok].add(yb * buf_w[:, None].astype(yb.dtype))
    return out[:n_tok]


def _n_layers_of(m):
    return len(range(m, DEPTH, N_MIXERS))


def setup_inputs(seed: int = 0) -> dict:
    key = jax.random.key(seed)
    keys = iter(jax.random.split(key, 64))

    def nrm(shape, std):
        return std * jax.random.normal(next(keys), shape, jnp.float32)

    D = D_MODEL
    n_a, n_b, n_c, n_d = (_n_layers_of(m) for m in range(N_MIXERS))
    a_cols = 4 * DA_HEADS * HEAD_DIM + DA_HEADS * DA_V_DIM
    b_cols = (SWA_HEADS + 2 * SWA_KV_HEADS) * HEAD_DIM
    c_cols = (GA_HEADS + 2 * GA_KV_HEADS) * HEAD_DIM
    d_cols = 3 * NA_HEADS * HEAD_DIM
    return {
        'x': nrm((BATCH, SEQ, D), 1.0),
        'c': nrm((BATCH, D), 1.0),
        'ctx': nrm((BATCH, CTX_LEN, D), 1.0),
        'c_ctx': nrm((D,), 1.0),
        'mod_w': nrm((DEPTH, D, 6 * D), 0.5 / math.sqrt(D)),
        'mod_b': nrm((DEPTH, 6 * D), 0.02),
        'norm1_g': 1.0 + nrm((DEPTH, D), 0.02),
        'norm2_g': 1.0 + nrm((DEPTH, D), 0.02),
        'a_w_qkv': nrm((n_a, D, a_cols), D ** -0.5),
        'a_w_o': nrm((n_a, DA_HEADS * DA_V_DIM, D), (DA_HEADS * DA_V_DIM) ** -0.5),
        'a_lam': nrm((n_a, 4, HEAD_DIM), 0.1),
        'a_subln_g': 1.0 + nrm((n_a, DA_V_DIM), 0.02),
        'b_w_qkv': nrm((n_b, D, b_cols), D ** -0.5),
        'b_w_o': nrm((n_b, SWA_HEADS * HEAD_DIM, D), (SWA_HEADS * HEAD_DIM) ** -0.5),
        'b_sink': nrm((n_b, SWA_HEADS), 0.5),
        'c_w_qkv': nrm((n_c, D, c_cols), D ** -0.5),
        'c_w_o': nrm((n_c, GA_HEADS * HEAD_DIM, D), (GA_HEADS * HEAD_DIM) ** -0.5),
        'c_qk_norm_g': 1.0 + nrm((n_c, 2, HEAD_DIM), 0.02),
        'd_w_qkv': nrm((n_d, D, d_cols), D ** -0.5),
        'd_w_o': nrm((n_d, NA_HEADS * HEAD_DIM, D), (NA_HEADS * HEAD_DIM) ** -0.5),
        'd_rpb': nrm((n_d, NA_HEADS, 2 * NA_ROWS - 1, 2 * NA_COLS - 1), 0.1),
        'moe_router_g': nrm((DEPTH, D, N_GROUPS), D ** -0.5),
        'moe_router_g_b': nrm((DEPTH, N_GROUPS), 0.01),
        'moe_router_e': nrm((DEPTH, D, N_EXPERTS), D ** -0.5),
        'moe_router_e_b': nrm((DEPTH, N_EXPERTS), 0.01),
        'moe_w_gate': nrm((DEPTH, N_EXPERTS, D, D_EXPERT), D ** -0.5),
        'moe_w_up': nrm((DEPTH, N_EXPERTS, D, D_EXPERT), D ** -0.5),
        'moe_w_down': nrm((DEPTH, N_EXPERTS, D_EXPERT, D), D_EXPERT ** -0.5),
        'final_g': 1.0 + nrm((D,), 0.02),
    }


def reference(x, c, ctx, c_ctx, mod_w, mod_b, norm1_g, norm2_g,
              a_w_qkv, a_w_o, a_lam, a_subln_g,
              b_w_qkv, b_w_o, b_sink,
              c_w_qkv, c_w_o, c_qk_norm_g,
              d_w_qkv, d_w_o, d_rpb,
              moe_router_g, moe_router_g_b, moe_router_e, moe_router_e_b,
              moe_w_gate, moe_w_up, moe_w_down, final_g):
    b, s, d = x.shape
    n_ctx = ctx.shape[1]
    cos, sin = _axial_rope_tables(s)
    xc = ctx
    for i in range(DEPTH):
        m, j = i % N_MIXERS, i // N_MIXERS
        need_ctx = i < DEPTH - 1
        mod = jax.nn.silu(c) @ mod_w[i] + mod_b[i]
        sh1, sc1, g1, sh2, sc2, g2 = jnp.split(mod[:, None, :], 6, axis=-1)
        mod_c = jax.nn.silu(c_ctx) @ mod_w[i] + mod_b[i]
        shc1, scc1, gc1, shc2, scc2, gc2 = jnp.split(mod_c, 6)
        h = _modulate(_rmsnorm(x, norm1_g[i]), sh1, sc1)
        hc = _modulate(_rmsnorm(xc, norm1_g[i]), shc1, scc1)
        if m == 0:
            lam_init = 0.8 - 0.6 * math.exp(-0.3 * i)
            y, yc = _diff_mixer(h, hc, a_w_qkv[j], a_w_o[j], a_lam[j], a_subln_g[j], lam_init, cos, sin, need_ctx)
        elif m == 1:
            y, yc = _swa_mixer(h, hc, b_w_qkv[j], b_w_o[j], b_sink[j], cos, sin, need_ctx)
        elif m == 2:
            y, yc = _axial_gqa_mixer(h, hc, c_w_qkv[j], c_w_o[j], c_qk_norm_g[j], cos, sin, need_ctx)
        else:
            y, yc = _na_mixer(h, hc, d_w_qkv[j], d_w_o[j], d_rpb[j], need_ctx)
        x = x + g1 * y
        h2 = _modulate(_rmsnorm(x, norm2_g[i]), sh2, sc2).reshape(b * s, d)
        if need_ctx:
            xc = xc + gc1 * yc
            h2c = _modulate(_rmsnorm(xc, norm2_g[i]), shc2, scc2).reshape(b * n_ctx, d)
            tokens = jnp.concatenate([h2, h2c], axis=0)
        else:
            tokens = h2
        f = _hier_moe(tokens, moe_router_g[i], moe_router_g_b[i], moe_router_e[i], moe_router_e_b[i],
                      moe_w_gate[i], moe_w_up[i], moe_w_down[i])
        x = x + g2 * f[:b * s].reshape(b, s, d)
        if need_ctx:
            xc = xc + gc2 * f[b * s:].reshape(b, n_ctx, d)
    return _rmsnorm(x, final_g)
```

```python
import functools
import math

import numpy as np
import jax
import jax.numpy as jnp
from jax import lax
from jax.experimental import pallas as pl
from jax.experimental.pallas import tpu as pltpu

GRID_W = 64
HEAD_DIM = 64
ROPE_THETA = 10000.0
NORM_EPS = 1e-6
NEG_INF = -1e30
N_GROUPS = 4
EXPERTS_PER_GROUP = 8
N_EXPERTS = N_GROUPS * EXPERTS_PER_GROUP
NA_ROWS = 8
NA_COLS = 16
SWA_WINDOW = 128

LANES = 128
ROW_TILE = 256
CHUNK = 256
QBLK = 128
MOE_TILE = 256
VMEM_LIMIT = 56 * 1024 * 1024

F32 = jnp.float32
BF16 = jnp.bfloat16
HIGHEST = lax.Precision.HIGHEST


def _params(sem, vmem=VMEM_LIMIT):
    return pltpu.CompilerParams(dimension_semantics=sem, vmem_limit_bytes=vmem)


def _mod_kernel(c_ref, w_ref, b_ref, o_ref):
    c = c_ref[...]
    s = c * (1.0 / (1.0 + jnp.exp(-c)))
    o_ref[0] = jnp.dot(s, w_ref[0], precision=HIGHEST, preferred_element_type=F32) + b_ref[0]


def _modulation(c2, mod_w, mod_b):
    depth, d, d6 = mod_w.shape
    return pl.pallas_call(
        _mod_kernel,
        grid=(depth, d6 // d),
        in_specs=[
            pl.BlockSpec((8, d), lambda i, j: (0, 0)),
            pl.BlockSpec((1, d, d), lambda i, j: (i, 0, j)),
            pl.BlockSpec((1, 1, d), lambda i, j: (i, 0, j)),
        ],
        out_specs=pl.BlockSpec((1, 8, d), lambda i, j: (i, 0, j)),
        out_shape=jax.ShapeDtypeStruct((depth, 8, d6), F32),
        compiler_params=_params(("parallel", "parallel")),
        name="modulation",
    )(c2, mod_w, mod_b.reshape(depth, 1, d6))


def _swap_pairs(a):
    lane = lax.broadcasted_iota(jnp.int32, (1, LANES), 1)
    first = (lane & 31) < 16
    parts = []
    for c in range(a.shape[-1] // LANES):
        ac = a[:, c * LANES:(c + 1) * LANES]
        parts.append(jnp.where(first, pltpu.roll(ac, LANES - 16, 1), pltpu.roll(ac, 16, 1)))
    return jnp.concatenate(parts, axis=1)


def _proj_kernel(x_ref, mod_ref, g_ref, w_ref, cs_ref, sn_ref, qkg_ref, avg_ref, qt_ref, k_ref, vt_ref,
                 *, n_lat_tiles, nq, nk, nv, rope, qknorm, d):
    is_ctx = pl.program_id(0) >= n_lat_tiles
    x = x_ref[...]
    y = x * lax.rsqrt(jnp.mean(x * x, axis=-1, keepdims=True) + NORM_EPS) * g_ref[...]
    mod = mod_ref[0]
    row = jnp.where(is_ctx, mod[1:2], mod[0:1])
    h = (y * (1.0 + row[:, d:2 * d]) + row[:, 0:d]).astype(BF16)
    for c in range((nq + nk + nv) // CHUNK):
        col = c * CHUNK
        a = jnp.dot(h, w_ref[:, col:col + CHUNK], preferred_element_type=F32)
        if col < nq + nk:
            is_q = col < nq
            if qknorm:
                a2 = a * a
                hi = a2.astype(BF16)
                lo = (a2 - hi.astype(F32)).astype(BF16)
                ms = (jnp.dot(hi, avg_ref[...], preferred_element_type=F32)
                      + jnp.dot(lo, avg_ref[...], preferred_element_type=F32))
                gain = qkg_ref[0:1] if is_q else qkg_ref[1:2]
                a = a * lax.rsqrt(ms + NORM_EPS) * gain
            if rope:
                a = a * cs_ref[...] + _swap_pairs(a) * sn_ref[...]
            if is_q:
                qt_ref[col:col + CHUNK, :] = (a * (1.0 / math.sqrt(HEAD_DIM))).T.astype(BF16)
            else:
                k_ref[:, col - nq:col - nq + CHUNK] = a.astype(BF16)
        else:
            vt_ref[col - nq - nk:col - nq - nk + CHUNK, :] = a.T.astype(BF16)


def _project(x, mod, layer, g, w, cs, sn, qkg, avg, *, n_lat, nq, nk, nv, rope, qknorm):
    t, d = x.shape
    n = nq + nk + nv
    kern = functools.partial(_proj_kernel, n_lat_tiles=n_lat // ROW_TILE, nq=nq, nk=nk, nv=nv,
                             rope=rope, qknorm=qknorm, d=d)
    return pl.pallas_call(
        kern,
        grid=(t // ROW_TILE,),
        in_specs=[
            pl.BlockSpec((ROW_TILE, d), lambda i: (i, 0)),
            pl.BlockSpec((1, 8, 6 * d), lambda i: (layer, 0, 0)),
            pl.BlockSpec((1, d), lambda i: (0, 0)),
            pl.BlockSpec((d, n), lambda i: (0, 0)),
            pl.BlockSpec((ROW_TILE, CHUNK), lambda i: (i, 0)),
            pl.BlockSpec((ROW_TILE, CHUNK), lambda i: (i, 0)),
            pl.BlockSpec((2, CHUNK), lambda i: (0, 0)),
            pl.BlockSpec((CHUNK, CHUNK), lambda i: (0, 0)),
        ],
        out_specs=[
            pl.BlockSpec((nq, ROW_TILE), lambda i: (0, i)),
            pl.BlockSpec((ROW_TILE, nk), lambda i: (i, 0)),
            pl.BlockSpec((nv, ROW_TILE), lambda i: (0, i)),
        ],
        out_shape=[
            jax.ShapeDtypeStruct((nq, t), BF16),
            jax.ShapeDtypeStruct((t, nk), BF16),
            jax.ShapeDtypeStruct((nv, t), BF16),
        ],
        compiler_params=_params(("parallel",)),
        name="qkv_proj",
    )(x, mod, g, w, cs, sn, qkg, avg)


def _masked_q(qt_ref, qm_ref, n_heads, gk):
    for h in range(n_heads):
        qh = qt_ref[h * HEAD_DIM:(h + 1) * HEAD_DIM, :]
        z = jnp.zeros_like(qh)
        qm_ref[h] = jnp.concatenate([qh, z] if (h // gk) % 2 == 0 else [z, qh], axis=0)


def _store_heads(o_ref, acc, inv_l, n_heads, dv, lam):
    tq = acc.shape[1]
    if lam is None:
        for p in range(n_heads * dv // LANES):
            per = LANES // dv
            scale = jnp.concatenate(
                [jnp.broadcast_to(inv_l[p * per + e:p * per + e + 1], (dv, tq)) for e in range(per)], axis=0)
            o_ref[:, p * LANES:(p + 1) * LANES] = (acc[p * LANES:(p + 1) * LANES] * scale).T.astype(o_ref.dtype)
    else:
        for h in range(n_heads // 2):
            a = acc[(2 * h) * dv:(2 * h + 1) * dv] * inv_l[2 * h:2 * h + 1]
            b = acc[(2 * h + 1) * dv:(2 * h + 2) * dv] * inv_l[2 * h + 1:2 * h + 2]
            o_ref[:, h * dv:(h + 1) * dv] = (a - lam * b).T.astype(o_ref.dtype)


def _diff_lambda(lam_ref, lam_init):
    lv = lam_ref[...]
    s1 = jnp.sum(lv[0:1] * lv[1:2], axis=-1, keepdims=True)
    s2 = jnp.sum(lv[2:3] * lv[3:4], axis=-1, keepdims=True)
    return jnp.exp(s1) - jnp.exp(s2) + lam_init


def _flash_kernel(qt_ref, k_ref, vt_ref, sink_ref, lam_ref, o_ref, qm_ref, m_ref, l_ref, acc_ref,
                  *, n_heads, gk, gv, dv, n_kt, has_sink, lam_init):
    j = pl.program_id(1)
    tq = qt_ref.shape[1]

    @pl.when(j == 0)
    def _():
        _masked_q(qt_ref, qm_ref, n_heads, gk)
        acc_ref[...] = jnp.zeros_like(acc_ref)
        if has_sink:
            for h in range(n_heads):
                m_ref[h:h + 1, :] = jnp.full((1, tq), sink_ref[h], F32)
            l_ref[...] = jnp.ones_like(l_ref)
        else:
            m_ref[...] = jnp.full_like(m_ref, NEG_INF)
            l_ref[...] = jnp.zeros_like(l_ref)

    for h in range(n_heads):
        pair = (h // gk) // 2
        s = jnp.dot(k_ref[:, pair * LANES:(pair + 1) * LANES], qm_ref[h], preferred_element_type=F32)
        m_old = m_ref[h:h + 1, :]
        m_new = jnp.maximum(m_old, jnp.max(s, axis=0, keepdims=True))
        alpha = jnp.exp(m_old - m_new)
        p = jnp.exp(s - m_new)
        l_ref[h:h + 1, :] = alpha * l_ref[h:h + 1, :] + jnp.sum(p, axis=0, keepdims=True)
        m_ref[h:h + 1, :] = m_new
        vh = h // gv
        pv = jnp.dot(vt_ref[vh * dv:(vh + 1) * dv, :], p.astype(BF16), preferred_element_type=F32)
        acc_ref[h * dv:(h + 1) * dv, :] = alpha * acc_ref[h * dv:(h + 1) * dv, :] + pv

    @pl.when(j == n_kt - 1)
    def _():
        lam = None if lam_init is None else _diff_lambda(lam_ref, lam_init)
        _store_heads(o_ref, acc_ref, 1.0 / l_ref[...], n_heads, dv, lam)


def _flash(qt, k, vt, sink, lam_vecs, *, q0, sq, k0, sk, tq, tk, n_heads, gk, gv, dv, has_sink, lam_init):
    nq_rows, nk_cols, nv_rows = qt.shape[0], k.shape[1], vt.shape[0]
    d_out = n_heads * dv if lam_init is None else n_heads * dv // 2
    n_kt = sk // tk
    kern = functools.partial(_flash_kernel, n_heads=n_heads, gk=gk, gv=gv, dv=dv, n_kt=n_kt,
                             has_sink=has_sink, lam_init=lam_init)
    return pl.pallas_call(
        kern,
        grid=(sq // tq, n_kt),
        in_specs=[
            pl.BlockSpec((nq_rows, tq), lambda i, j: (0, q0 // tq + i)),
            pl.BlockSpec((tk, nk_cols), lambda i, j: (k0 // tk + j, 0)),
            pl.BlockSpec((nv_rows, tk), lambda i, j: (0, k0 // tk + j)),
            pl.BlockSpec(memory_space=pltpu.SMEM),
            pl.BlockSpec(lam_vecs.shape, lambda i, j: (0, 0)),
        ],
        out_specs=pl.BlockSpec((tq, d_out), lambda i, j: (i, 0)),
        out_shape=jax.ShapeDtypeStruct((sq, d_out), BF16),
        scratch_shapes=[
            pltpu.VMEM((n_heads, LANES, tq), BF16),
            pltpu.VMEM((n_heads, tq), F32),
            pltpu.VMEM((n_heads, tq), F32),
            pltpu.VMEM((n_heads * dv, tq), F32),
        ],
        compiler_params=_params(("parallel", "arbitrary")),
        name="flash_attn",
    )(qt, k, vt, sink, lam_vecs)


def _window_kernel(ws_ref, ty_ref, qt_ref, *rest, n_heads, gk, n_win, hb):
    k_refs = rest[0:n_win]
    vt_refs = rest[n_win:2 * n_win]
    kc_ref, vct_ref, bias_ref, sink_ref, o_ref, qm_ref = rest[2 * n_win:]
    tq = qt_ref.shape[1]
    _masked_q(qt_ref, qm_ref, n_heads, gk)
    kwin = jnp.concatenate([r[...] for r in k_refs], axis=0)
    vwin = jnp.concatenate([r[...] for r in vt_refs], axis=1)
    accs, inv_ls = [], []
    for h in range(n_heads):
        g = h // gk
        pair = g // 2
        qm = qm_ref[h]
        s_c = jnp.dot(kc_ref[:, pair * LANES:(pair + 1) * LANES], qm, preferred_element_type=F32)
        s_w = jnp.dot(kwin[:, pair * LANES:(pair + 1) * LANES], qm, preferred_element_type=F32)
        s_w = s_w + bias_ref[0, h if hb > 1 else 0]
        sink = jnp.full((1, tq), sink_ref[h], F32)
        m = jnp.maximum(jnp.maximum(jnp.max(s_c, axis=0, keepdims=True), jnp.max(s_w, axis=0, keepdims=True)), sink)
        p_c = jnp.exp(s_c - m)
        p_w = jnp.exp(s_w - m)
        l = jnp.sum(p_c, axis=0, keepdims=True) + jnp.sum(p_w, axis=0, keepdims=True) + jnp.exp(sink - m)
        vrows = slice(g * HEAD_DIM, (g + 1) * HEAD_DIM)
        acc = (jnp.dot(vct_ref[vrows, :], p_c.astype(BF16), preferred_element_type=F32)
               + jnp.dot(vwin[vrows, :], p_w.astype(BF16), preferred_element_type=F32))
        accs.append(acc)
        inv_ls.append(1.0 / l)
    _store_heads(o_ref, jnp.concatenate(accs, axis=0), jnp.concatenate(inv_ls, axis=0), n_heads, HEAD_DIM, None)


def _windowed(qt, k, vt, bias, sink, wstart, wtype, *, n_lat, n_ctx, n_heads, gk, n_win):
    nq_rows, nk_cols, nv_rows = qt.shape[0], k.shape[1], vt.shape[0]
    hb, wlen = bias.shape[1], bias.shape[2]
    nb = n_lat // QBLK
    kern = functools.partial(_window_kernel, n_heads=n_heads, gk=gk, n_win=n_win, hb=hb)
    k_specs = [pl.BlockSpec((QBLK, nk_cols), functools.partial(lambda b, ws, ty, jj: (ws[b] + jj, 0), jj=jj))
               for jj in range(n_win)]
    v_specs = [pl.BlockSpec((nv_rows, QBLK), functools.partial(lambda b, ws, ty, jj: (0, ws[b] + jj), jj=jj))
               for jj in range(n_win)]
    grid_spec = pltpu.PrefetchScalarGridSpec(
        num_scalar_prefetch=2,
        grid=(nb,),
        in_specs=[pl.BlockSpec((nq_rows, QBLK), lambda b, ws, ty: (0, b))] + k_specs + v_specs + [
            pl.BlockSpec((n_ctx, nk_cols), lambda b, ws, ty: (n_lat // n_ctx, 0)),
            pl.BlockSpec((nv_rows, n_ctx), lambda b, ws, ty: (0, n_lat // n_ctx)),
            pl.BlockSpec((1, hb, wlen, QBLK), lambda b, ws, ty: (ty[b], 0, 0, 0)),
            pl.BlockSpec(memory_space=pltpu.SMEM),
        ],
        out_specs=pl.BlockSpec((QBLK, n_heads * HEAD_DIM), lambda b, ws, ty: (b, 0)),
        scratch_shapes=[pltpu.VMEM((n_heads, LANES, QBLK), BF16)],
    )
    return pl.pallas_call(
        kern,
        grid_spec=grid_spec,
        out_shape=jax.ShapeDtypeStruct((n_lat, n_heads * HEAD_DIM), BF16),
        compiler_params=_params(("arbitrary",)),
        name="window_attn",
    )(wstart, wtype, qt, *([k] * n_win), *([vt] * n_win), k, vt, bias, sink)


def _route(logits):
    lane = lax.broadcasted_iota(jnp.int32, logits.shape, 1).astype(F32)
    big = float(LANES)
    lg = jnp.where(lane < N_GROUPS, logits, NEG_INF)
    mg = jnp.max(lg, axis=1, keepdims=True)
    g = jnp.min(jnp.where(lg == mg, lane, big), axis=1, keepdims=True)
    gate = 1.0 / jnp.sum(jnp.exp(lg - mg), axis=1, keepdims=True)
    lo = N_GROUPS + EXPERTS_PER_GROUP * g
    le = jnp.where((lane >= lo) & (lane < lo + EXPERTS_PER_GROUP), logits, NEG_INF)
    v1 = jnp.max(le, axis=1, keepdims=True)
    i1 = jnp.min(jnp.where(le == v1, lane, big), axis=1, keepdims=True)
    le2 = jnp.where(lane == i1, NEG_INF, le)
    v2 = jnp.max(le2, axis=1, keepdims=True)
    i2 = jnp.min(jnp.where(le2 == v2, lane, big), axis=1, keepdims=True)
    e2 = jnp.exp(v2 - v1)
    den = 1.0 + e2
    w1 = gate * (1.0 / den)
    w2 = gate * (e2 / den)
    out = jnp.where(lane == 0, i1 - N_GROUPS, 0.0)
    out = jnp.where(lane == 1, i2 - N_GROUPS, out)
    out = jnp.where(lane == 2, w1, out)
    out = jnp.where(lane == 3, w2, out)
    return out


def _outproj_kernel(ol_ref, oc_ref, wo_ref, x_ref, mod_ref, g2_ref, sub_ref, wr_ref, br_ref,
                    xo_ref, h2_ref, rt_ref, *, n_lat_tiles, d, subln_scale):
    is_ctx = pl.program_id(0) >= n_lat_tiles
    o = jnp.where(is_ctx, oc_ref[...], ol_ref[...])
    if subln_scale is not None:
        parts = []
        for h in range(d // LANES):
            oh = o[:, h * LANES:(h + 1) * LANES].astype(F32)
            yh = oh * lax.rsqrt(jnp.mean(oh * oh, axis=-1, keepdims=True) + NORM_EPS)
            parts.append((yh * sub_ref[...] * subln_scale).astype(BF16))
        o = jnp.concatenate(parts, axis=1)
    y = jnp.dot(o, wo_ref[...], preferred_element_type=F32)
    mod = mod_ref[0]
    row = jnp.where(is_ctx, mod[1:2], mod[0:1])
    x = x_ref[...] + row[:, 2 * d:3 * d] * y
    xo_ref[...] = x
    n = x * lax.rsqrt(jnp.mean(x * x, axis=-1, keepdims=True) + NORM_EPS) * g2_ref[...]
    h2 = n * (1.0 + row[:, 4 * d:5 * d]) + row[:, 3 * d:4 * d]
    h2_ref[...] = h2
    logits = jnp.dot(h2, wr_ref[...], precision=HIGHEST, preferred_element_type=F32) + br_ref[...]
    rt_ref[...] = _route(logits)


def _outproj(o_lat, o_ctx, wo, x, mod, layer, g2, sub, wr, br, *, n_lat, subln_scale):
    t, d = x.shape
    assert o_ctx.shape[0] == ROW_TILE
    nl = n_lat // ROW_TILE
    kern = functools.partial(_outproj_kernel, n_lat_tiles=nl, d=d, subln_scale=subln_scale)
    return pl.pallas_call(
        kern,
        grid=(t // ROW_TILE,),
        in_specs=[
            pl.BlockSpec((ROW_TILE, d), lambda i: (jnp.minimum(i, nl - 1), 0)),
            pl.BlockSpec((ROW_TILE, d), lambda i: (0, 0)),
            pl.BlockSpec((d, d), lambda i: (0, 0)),
            pl.BlockSpec((ROW_TILE, d), lambda i: (i, 0)),
            pl.BlockSpec((1, 8, 6 * d), lambda i: (layer, 0, 0)),
            pl.BlockSpec((1, d), lambda i: (0, 0)),
            pl.BlockSpec((1, LANES), lambda i: (0, 0)),
            pl.BlockSpec((d, LANES), lambda i: (0, 0)),
            pl.BlockSpec((1, LANES), lambda i: (0, 0)),
        ],
        out_specs=[
            pl.BlockSpec((ROW_TILE, d), lambda i: (i, 0)),
            pl.BlockSpec((ROW_TILE, d), lambda i: (i, 0)),
            pl.BlockSpec((ROW_TILE, LANES), lambda i: (i, 0)),
        ],
        out_shape=[
            jax.ShapeDtypeStruct((t, d), F32),
            jax.ShapeDtypeStruct((t, d), F32),
            jax.ShapeDtypeStruct((t, LANES), F32),
        ],
        compiler_params=_params(("parallel",)),
        name="out_proj_router",
    )(o_lat, o_ctx, wo, x, mod, g2, sub, wr, br)


def _gather_rows(idx_ref, base, src_ref, dst_ref, sem, n_rows):
    def body(r, carry):
        tok = idx_ref[base + r]
        pltpu.make_async_copy(src_ref.at[pl.ds(tok, 1)], dst_ref.at[pl.ds(r, 1)], sem).start()
        return carry
    lax.fori_loop(0, n_rows, body, 0, unroll=8)


def _wait_rows(src_ref, dst_ref, sem, n_rows):
    pltpu.make_async_copy(src_ref.at[pl.ds(0, n_rows)], dst_ref, sem).wait()


def _expert_kernel(be_ref, tok_ref, nu_ref, h2_ref, wg_ref, wu_ref, wd_ref, y_ref, xbuf, sems):
    i = pl.program_id(0)
    n_used = nu_ref[0]
    slot = i % 2

    @pl.when(i == 0)
    def _():
        _gather_rows(tok_ref, 0, h2_ref, xbuf.at[0], sems.at[0], MOE_TILE)

    @pl.when(i + 1 < n_used)
    def _():
        _gather_rows(tok_ref, (i + 1) * MOE_TILE, h2_ref, xbuf.at[1 - slot], sems.at[1 - slot], MOE_TILE)

    @pl.when(i < n_used)
    def _():
        _wait_rows(h2_ref, xbuf.at[slot], sems.at[slot], MOE_TILE)
        x = xbuf[slot].astype(BF16)
        a = jnp.dot(x, wg_ref[...], preferred_element_type=F32)
        u = jnp.dot(x, wu_ref[...], preferred_element_type=F32)
        hid = (a * (1.0 / (1.0 + jnp.exp(-a))) * u).astype(BF16)
        y_ref[...] = jnp.dot(hid, wd_ref[...], preferred_element_type=F32)

    @pl.when(i >= n_used)
    def _():
        y_ref[...] = jnp.zeros_like(y_ref)


def _experts(blk_e, buf_tok, n_used, h2, wg, wu, wd):
    t, d = h2.shape
    n_blk = blk_e.shape[0]
    de = wg.shape[-1]
    grid_spec = pltpu.PrefetchScalarGridSpec(
        num_scalar_prefetch=3,
        grid=(n_blk,),
        in_specs=[
            pl.BlockSpec(memory_space=pl.ANY),
            pl.BlockSpec((None, d, de), lambda i, be, tk, nu: (be[i], 0, 0)),
            pl.BlockSpec((None, d, de), lambda i, be, tk, nu: (be[i], 0, 0)),
            pl.BlockSpec((None, de, d), lambda i, be, tk, nu: (be[i], 0, 0)),
        ],
        out_specs=pl.BlockSpec((MOE_TILE, d), lambda i, be, tk, nu: (i, 0)),
        scratch_shapes=[pltpu.VMEM((2, MOE_TILE, d), F32), pltpu.SemaphoreType.DMA((2,))],
    )
    return pl.pallas_call(
        _expert_kernel,
        grid_spec=grid_spec,
        out_shape=jax.ShapeDtypeStruct((n_blk * MOE_TILE, d), F32),
        compiler_params=_params(("arbitrary",)),
        name="moe_experts",
    )(blk_e, buf_tok, n_used, h2, wg, wu, wd)


def _combine_kernel(p0_ref, p1_ref, yb_ref, x_ref, rt_ref, mod_ref, xo_ref, abuf, bbuf, sems,
                    *, n_lat_tiles, n_tiles, d):
    i = pl.program_id(0)
    slot = i % 2

    def start(tile, s):
        _gather_rows(p0_ref, tile * ROW_TILE, yb_ref, abuf.at[s], sems.at[0, s], ROW_TILE)
        _gather_rows(p1_ref, tile * ROW_TILE, yb_ref, bbuf.at[s], sems.at[1, s], ROW_TILE)

    @pl.when(i == 0)
    def _():
        start(0, 0)

    @pl.when(i + 1 < n_tiles)
    def _():
        start(i + 1, 1 - slot)

    _wait_rows(yb_ref, abuf.at[slot], sems.at[0, slot], ROW_TILE)
    _wait_rows(yb_ref, bbuf.at[slot], sems.at[1, slot], ROW_TILE)
    rt = rt_ref[...]
    f = rt[:, 2:3] * abuf[slot] + rt[:, 3:4] * bbuf[slot]
    mod = mod_ref[0]
    row = jnp.where(i >= n_lat_tiles, mod[1:2], mod[0:1])
    xo_ref[...] = x_ref[...] + row[:, 5 * d:6 * d] * f


def _combine(pos0, pos1, yb, x, route, mod, layer, *, n_lat):
    t, d = x.shape
    n_tiles = t // ROW_TILE
    kern = functools.partial(_combine_kernel, n_lat_tiles=n_lat // ROW_TILE, n_tiles=n_tiles, d=d)
    grid_spec = pltpu.PrefetchScalarGridSpec(
        num_scalar_prefetch=2,
        grid=(n_tiles,),
        in_specs=[
            pl.BlockSpec(memory_space=pl.ANY),
            pl.BlockSpec((ROW_TILE, d), lambda i, p0, p1: (i, 0)),
            pl.BlockSpec((ROW_TILE, LANES), lambda i, p0, p1: (i, 0)),
            pl.BlockSpec((1, 8, 6 * d), lambda i, p0, p1: (layer, 0, 0)),
        ],
        out_specs=pl.BlockSpec((ROW_TILE, d), lambda i, p0, p1: (i, 0)),
        scratch_shapes=[pltpu.VMEM((2, ROW_TILE, d), F32), pltpu.VMEM((2, ROW_TILE, d), F32),
                        pltpu.SemaphoreType.DMA((2, 2))],
    )
    return pl.pallas_call(
        kern,
        grid_spec=grid_spec,
        out_shape=jax.ShapeDtypeStruct((t, d), F32),
        compiler_params=_params(("arbitrary",)),
        name="moe_combine",
    )(pos0, pos1, yb, x, route, mod)


def _final_kernel(x_ref, g_ref, o_ref):
    x = x_ref[...]
    o_ref[...] = x * lax.rsqrt(jnp.mean(x * x, axis=-1, keepdims=True) + NORM_EPS) * g_ref[...]


def _final_norm(x, g, n_lat):
    d = x.shape[1]
    return pl.pallas_call(
        _final_kernel,
        grid=(n_lat // ROW_TILE,),
        in_specs=[pl.BlockSpec((ROW_TILE, d), lambda i: (i, 0)), pl.BlockSpec((1, d), lambda i: (0, 0))],
        out_specs=pl.BlockSpec((ROW_TILE, d), lambda i: (i, 0)),
        out_shape=jax.ShapeDtypeStruct((n_lat, d), F32),
        compiler_params=_params(("parallel",)),
        name="final_norm",
    )(x, g)


def _rope_tables(n_lat, n_ctx):
    t = jnp.arange(n_lat, dtype=jnp.int32)
    pos = jnp.stack([t // GRID_W, t % GRID_W], axis=-1).astype(F32)
    quarter = HEAD_DIM // 4
    inv_freq = 1.0 / (ROPE_THETA ** (jnp.arange(quarter, dtype=F32) / quarter))
    ang = pos[:, :, None] * inv_freq
    cos, sin = jnp.cos(ang), jnp.sin(ang)
    cs = jnp.stack([cos, cos], axis=2).reshape(n_lat, HEAD_DIM)
    sn = jnp.stack([-sin, sin], axis=2).reshape(n_lat, HEAD_DIM)
    cs = jnp.concatenate([cs, jnp.ones((n_ctx, HEAD_DIM), F32)], axis=0)
    sn = jnp.concatenate([sn, jnp.zeros((n_ctx, HEAD_DIM), F32)], axis=0)
    rep = CHUNK // HEAD_DIM
    return jnp.tile(cs, (1, rep)), jnp.tile(sn, (1, rep))


def _swa_tables(n_lat):
    nb = n_lat // QBLK
    n_win = 3
    wstart = np.clip(np.arange(nb) - 1, 0, nb - n_win).astype(np.int32)
    deltas = sorted(set((wstart - np.arange(nb)).tolist()))
    wtype = np.array([deltas.index(d) for d in (wstart - np.arange(nb))], np.int32)
    kj = np.arange(n_win * QBLK)[:, None]
    qi = np.arange(QBLK)[None, :]
    bias = np.stack([np.where(np.abs(dl * QBLK + kj - qi) <= SWA_WINDOW, 0.0, NEG_INF) for dl in deltas])
    return wstart, wtype, bias[:, None].astype(np.float32), n_win


def _na_tables(n_lat):
    rows = n_lat // GRID_W
    kh = min(NA_ROWS, rows)
    rq = QBLK // GRID_W
    nb = n_lat // QBLK
    wrows = kh + rq
    n_win = wrows // rq
    b = np.arange(nb)
    wstart_rows = np.clip(rq * b - kh // 2, 0, rows - wrows)
    wstart = (wstart_rows // rq).astype(np.int32)
    assert np.all(wstart_rows % rq == 0)
    deltas = sorted(set((wstart_rows - rq * b).tolist()))
    wtype = np.array([deltas.index(d) for d in (wstart_rows - rq * b)], np.int32)
    wj = np.arange(wrows)[:, None, None, None]
    kc = np.arange(GRID_W)[None, :, None, None]
    a = np.arange(rq)[None, None, :, None]
    c = np.arange(GRID_W)[None, None, None, :]
    c0 = np.clip(c - NA_COLS // 2, 0, GRID_W - NA_COLS)
    col_ok = (kc >= c0) & (kc < c0 + NA_COLS)
    dc = np.broadcast_to(kc - c + (NA_COLS - 1), (wrows, GRID_W, rq, GRID_W))
    idx_r, idx_c, valid = [], [], []
    for dl in deltas:
        blk = b[list(wstart_rows - rq * b).index(dl)]
        r = rq * blk + a
        kr = wstart_rows[blk] + wj
        r0 = np.clip(r - kh // 2, 0, rows - kh)
        ok = np.broadcast_to((kr >= r0) & (kr < r0 + kh) & col_ok, dc.shape)
        dr = np.broadcast_to(kr - r + (NA_ROWS - 1), dc.shape)
        idx_r.append(np.where(ok, dr, 0))
        idx_c.append(np.where(ok, dc, 0))
        valid.append(ok)
    shape = (len(deltas), wrows * GRID_W, QBLK)
    return (wstart, wtype, np.stack(idx_r).reshape(shape), np.stack(idx_c).reshape(shape),
            np.stack(valid).reshape(shape), n_win)


def _na_bias(rpb, idx_r, idx_c, valid):
    g = rpb.astype(F32)[:, idx_r, idx_c]
    return jnp.where(valid[None], g, NEG_INF).transpose(1, 0, 2, 3)


def _dispatch_tables(route, n_tok):
    expert = route[:, 0:2].astype(jnp.int32).reshape(-1)
    n_assign = n_tok * 2
    n_blk = -(-n_assign // MOE_TILE) + N_EXPERTS
    order = jnp.argsort(expert, stable=True).astype(jnp.int32)
    e_sorted = expert[order]
    counts = jnp.zeros((N_EXPERTS,), jnp.int32).at[expert].add(1)
    padded = (counts + MOE_TILE - 1) // MOE_TILE * MOE_TILE
    p_end = jnp.cumsum(padded)
    p_start = p_end - padded
    start = jnp.cumsum(counts) - counts
    dest = p_start[e_sorted] + jnp.arange(n_assign, dtype=jnp.int32) - start[e_sorted]
    buf_tok = jnp.zeros((n_blk * MOE_TILE,), jnp.int32).at[dest].set(order // 2)
    pos = jnp.zeros((n_assign,), jnp.int32).at[order].set(dest)
    n_used = (p_end[-1] // MOE_TILE).astype(jnp.int32)
    blk = jnp.minimum(jnp.arange(n_blk, dtype=jnp.int32), n_used - 1) * MOE_TILE
    blk_e = jnp.clip(jnp.searchsorted(p_end, blk, side='right'), 0, N_EXPERTS - 1).astype(jnp.int32)
    return blk_e, buf_tok, n_used.reshape(1), pos[0::2], pos[1::2]


def kernel(x, c, ctx, c_ctx, mod_w, mod_b, norm1_g, norm2_g, a_w_qkv, a_w_o, a_lam, a_subln_g, b_w_qkv, b_w_o,
           b_sink, c_w_qkv, c_w_o, c_qk_norm_g, d_w_qkv, d_w_o, d_rpb, moe_router_g, moe_router_g_b,
           moe_router_e, moe_router_e_b, moe_w_gate, moe_w_up, moe_w_down, final_g):
    b, n_lat, d = x.shape
    n_ctx = ctx.shape[1]
    depth = mod_w.shape[0]
    assert b == 1 and n_ctx == ROW_TILE and n_lat % (2 * QBLK) == 0 and d % CHUNK == 0
    t = n_lat + n_ctx
    xs = jnp.concatenate([x[0], ctx[0]], axis=0)

    c2 = jnp.zeros((8, d), F32).at[0].set(c[0]).at[1].set(c_ctx)
    mod = _modulation(c2, mod_w, mod_b)

    cs, sn = _rope_tables(n_lat, n_ctx)
    ones_qkg = jnp.ones((2, CHUNK), F32)
    ii = np.arange(CHUNK)
    avg = jnp.asarray((ii[:, None] // HEAD_DIM == ii[None, :] // HEAD_DIM) / HEAD_DIM, BF16)
    no_sink = jnp.full((16,), NEG_INF, F32)
    no_lam = jnp.zeros((4, HEAD_DIM), F32)
    no_sub = jnp.ones((1, LANES), F32)
    n_heads = d // HEAD_DIM

    for i in range(depth):
        m, j = i % 4, i // 4
        g1 = norm1_g[i].reshape(1, d)
        if m == 0:
            nq, nk, nv = d, d, d
            w, wo = a_w_qkv[j], a_w_o[j]
        elif m == 1:
            nq, nk, nv = d, 4 * HEAD_DIM, 4 * HEAD_DIM
            w, wo = b_w_qkv[j], b_w_o[j]
        elif m == 2:
            nq, nk, nv = d, 4 * HEAD_DIM, 4 * HEAD_DIM
            w, wo = c_w_qkv[j], c_w_o[j]
        else:
            nq, nk, nv = d, d, d
            w, wo = d_w_qkv[j], d_w_o[j]
        qkg = jnp.tile(c_qk_norm_g[j], (1, CHUNK // HEAD_DIM)) if m == 2 else ones_qkg
        qt, k, vt = _project(xs, mod, i, g1, w.astype(BF16), cs, sn, qkg, avg, n_lat=n_lat, nq=nq, nk=nk, nv=nv,
                             rope=(m != 3), qknorm=(m == 2))

        subln_scale = None
        sub = no_sub
        if m == 0:
            lam_init = 0.8 - 0.6 * math.exp(-0.3 * i)
            kw = dict(n_heads=n_heads, gk=1, gv=2, dv=2 * HEAD_DIM, has_sink=False, lam_init=lam_init)
            o_lat = _flash(qt, k, vt, no_sink, a_lam[j], q0=0, sq=n_lat, k0=0, sk=t, tq=512, tk=640, **kw)
            o_ctx = _flash(qt, k, vt, no_sink, a_lam[j], q0=n_lat, sq=n_ctx, k0=n_lat, sk=n_ctx, tq=n_ctx,
                           tk=n_ctx, **kw)
            subln_scale = 1.0 - lam_init
            sub = a_subln_g[j].reshape(1, LANES)
        elif m == 2:
            kw = dict(n_heads=n_heads, gk=4, gv=4, dv=HEAD_DIM, has_sink=False, lam_init=None)
            o_lat = _flash(qt, k, vt, no_sink, no_lam, q0=0, sq=n_lat, k0=0, sk=t, tq=512, tk=640, **kw)
            o_ctx = _flash(qt, k, vt, no_sink, no_lam, q0=n_lat, sq=n_ctx, k0=n_lat, sk=n_ctx, tq=n_ctx,
                           tk=n_ctx, **kw)
        else:
            if m == 1:
                wstart, wtype, bias, n_win = _swa_tables(n_lat)
                bias = jnp.asarray(bias)
                sink, gk = b_sink[j], 4
            else:
                wstart, wtype, idx_r, idx_c, valid, n_win = _na_tables(n_lat)
                bias = _na_bias(d_rpb[j], idx_r, idx_c, valid)
                sink, gk = no_sink, 1
            o_lat = _windowed(qt, k, vt, bias, sink, jnp.asarray(wstart), jnp.asarray(wtype), n_lat=n_lat,
                              n_ctx=n_ctx, n_heads=n_heads, gk=gk, n_win=n_win)
            o_ctx = _flash(qt, k, vt, sink, no_lam, q0=n_lat, sq=n_ctx, k0=n_lat, sk=n_ctx, tq=n_ctx, tk=n_ctx,
                           n_heads=n_heads, gk=gk, gv=gk, dv=HEAD_DIM, has_sink=(m == 1), lam_init=None)

        wr = jnp.zeros((d, LANES), F32).at[:, 0:N_GROUPS].set(moe_router_g[i])
        wr = wr.at[:, N_GROUPS:N_GROUPS + N_EXPERTS].set(moe_router_e[i])
        br = jnp.zeros((1, LANES), F32).at[0, 0:N_GROUPS].set(moe_router_g_b[i])
        br = br.at[0, N_GROUPS:N_GROUPS + N_EXPERTS].set(moe_router_e_b[i])
        xs, h2, route = _outproj(o_lat, o_ctx, wo.astype(BF16), xs, mod, i, norm2_g[i].reshape(1, d), sub, wr, br,
                                 n_lat=n_lat, subln_scale=subln_scale)

        blk_e, buf_tok, n_used, pos0, pos1 = _dispatch_tables(route, t)
        yb = _experts(blk_e, buf_tok, n_used, h2, moe_w_gate[i].astype(BF16), moe_w_up[i].astype(BF16),
                      moe_w_down[i].astype(BF16))
        xs = _combine(pos0, pos1, yb, xs, route, mod, i, n_lat=n_lat)

    return _final_norm(xs, final_g.reshape(1, d), n_lat)[None]
```

```python
import functools
import math

import numpy as np
import jax
import jax.numpy as jnp
from jax import lax
from jax.experimental import pallas as pl
from jax.experimental.pallas import tpu as pltpu

GRID_W = 64
HEAD_DIM = 64
ROPE_THETA = 10000.0
NORM_EPS = 1e-6
NEG_INF = -1e30
N_GROUPS = 4
EXPERTS_PER_GROUP = 8
N_EXPERTS = N_GROUPS * EXPERTS_PER_GROUP
NA_ROWS = 8
NA_COLS = 16
SWA_WINDOW = 128

LANES = 128
ROW_TILE = 256
CHUNK = 256
QBLK = 128
MOE_TILE = 256
VMEM_LIMIT = 56 * 1024 * 1024
SUM_ROWS = 16
LOG2E = math.log2(math.e)
FLASH_TQ = 512
FLASH_TK = 1280

F32 = jnp.float32
BF16 = jnp.bfloat16
HIGHEST = lax.Precision.HIGHEST


def _params(sem, vmem=VMEM_LIMIT):
    return pltpu.CompilerParams(dimension_semantics=sem, vmem_limit_bytes=vmem)


def _mod_kernel(c_ref, w_ref, b_ref, o_ref):
    c = c_ref[...]
    s = c * (1.0 / (1.0 + jnp.exp(-c)))
    o_ref[0] = jnp.dot(s, w_ref[0], precision=HIGHEST, preferred_element_type=F32) + b_ref[0]


def _modulation(c2, mod_w, mod_b):
    depth, d, d6 = mod_w.shape
    return pl.pallas_call(
        _mod_kernel,
        grid=(depth, d6 // d),
        in_specs=[
            pl.BlockSpec((8, d), lambda i, j: (0, 0)),
            pl.BlockSpec((1, d, d), lambda i, j: (i, 0, j)),
            pl.BlockSpec((1, 1, d), lambda i, j: (i, 0, j)),
        ],
        out_specs=pl.BlockSpec((1, 8, d), lambda i, j: (i, 0, j)),
        out_shape=jax.ShapeDtypeStruct((depth, 8, d6), F32),
        compiler_params=_params(("parallel", "parallel")),
        name="modulation",
    )(c2, mod_w, mod_b.reshape(depth, 1, d6))


def _swap_pairs(a):
    lane = lax.broadcasted_iota(jnp.int32, (1, LANES), 1)
    first = (lane & 31) < 16
    parts = []
    for c in range(a.shape[-1] // LANES):
        ac = a[:, c * LANES:(c + 1) * LANES]
        parts.append(jnp.where(first, pltpu.roll(ac, LANES - 16, 1), pltpu.roll(ac, 16, 1)))
    return jnp.concatenate(parts, axis=1)


def _proj_kernel(x_ref, mod_ref, g_ref, w_ref, cs_ref, sn_ref, qkg_ref, avg_ref, qt_ref, k_ref, vt_ref,
                 *, n_lat_tiles, nq, nk, nv, dv, gk, rope, qknorm, d):
    is_ctx = pl.program_id(0) >= n_lat_tiles
    x = x_ref[...]
    y = x * lax.rsqrt(jnp.mean(x * x, axis=-1, keepdims=True) + NORM_EPS) * g_ref[...]
    mod = mod_ref[0]
    row = jnp.where(is_ctx, mod[1:2], mod[0:1])
    h = (y * (1.0 + row[:, d:2 * d]) + row[:, 0:d]).astype(BF16)
    for c in range((nq + nk + nv) // CHUNK):
        col = c * CHUNK
        a = jnp.dot(h, w_ref[:, col:col + CHUNK], preferred_element_type=F32)
        if col < nq + nk:
            is_q = col < nq
            if qknorm:
                a2 = a * a
                hi = a2.astype(BF16)
                lo = (a2 - hi.astype(F32)).astype(BF16)
                ms = (jnp.dot(hi, avg_ref[...], preferred_element_type=F32)
                      + jnp.dot(lo, avg_ref[...], preferred_element_type=F32))
                gain = qkg_ref[0:1] if is_q else qkg_ref[1:2]
                a = a * lax.rsqrt(ms + NORM_EPS) * gain
            if rope:
                a = a * cs_ref[...] + _swap_pairs(a) * sn_ref[...]
            if is_q:
                at = (a * (LOG2E / math.sqrt(HEAD_DIM))).T.astype(BF16)
                zeros = jnp.zeros((HEAD_DIM, at.shape[1]), BF16)
                for e in range(CHUNK // HEAD_DIM):
                    base = (col // HEAD_DIM + e) * LANES
                    half = ((col // HEAD_DIM + e) // gk) % 2
                    qt_ref[base + half * HEAD_DIM:base + (half + 1) * HEAD_DIM, :] = at[e * HEAD_DIM:(e + 1) * HEAD_DIM]
                    qt_ref[base + (1 - half) * HEAD_DIM:base + (2 - half) * HEAD_DIM, :] = zeros
            else:
                k_ref[:, col - nq:col - nq + CHUNK] = a.astype(BF16)
        else:
            at = a.T.astype(BF16)
            ones = (lax.broadcasted_iota(jnp.int32, (SUM_ROWS, at.shape[1]), 0) == 0).astype(BF16)
            for e in range(CHUNK // dv):
                base = ((col - nq - nk) // dv + e) * (dv + SUM_ROWS)
                vt_ref[base:base + dv, :] = at[e * dv:(e + 1) * dv]
                vt_ref[base + dv:base + dv + SUM_ROWS, :] = ones


def _project(x, mod, layer, g, w, cs, sn, qkg, avg, *, n_lat, nq, nk, nv, dv, gk, rope, qknorm):
    t, d = x.shape
    n = nq + nk + nv
    nv_ext = nv // dv * (dv + SUM_ROWS)
    kern = functools.partial(_proj_kernel, n_lat_tiles=n_lat // ROW_TILE, nq=nq, nk=nk, nv=nv, dv=dv,
                             gk=gk, rope=rope, qknorm=qknorm, d=d)
    return pl.pallas_call(
        kern,
        grid=(t // ROW_TILE,),
        in_specs=[
            pl.BlockSpec((ROW_TILE, d), lambda i: (i, 0)),
            pl.BlockSpec((1, 8, 6 * d), lambda i: (layer, 0, 0)),
            pl.BlockSpec((1, d), lambda i: (0, 0)),
            pl.BlockSpec((d, n), lambda i: (0, 0)),
            pl.BlockSpec((ROW_TILE, CHUNK), lambda i: (i, 0)),
            pl.BlockSpec((ROW_TILE, CHUNK), lambda i: (i, 0)),
            pl.BlockSpec((2, CHUNK), lambda i: (0, 0)),
            pl.BlockSpec((CHUNK, CHUNK), lambda i: (0, 0)),
        ],
        out_specs=[
            pl.BlockSpec((2 * nq, ROW_TILE), lambda i: (0, i)),
            pl.BlockSpec((ROW_TILE, nk), lambda i: (i, 0)),
            pl.BlockSpec((nv_ext, ROW_TILE), lambda i: (0, i)),
        ],
        out_shape=[
            jax.ShapeDtypeStruct((2 * nq, t), BF16),
            jax.ShapeDtypeStruct((t, nk), BF16),
            jax.ShapeDtypeStruct((nv_ext, t), BF16),
        ],
        compiler_params=_params(("parallel",)),
        name="qkv_proj",
    )(x, mod, g, w, cs, sn, qkg, avg)


def _store_heads(o_ref, head, n_heads, dv, lam):
    def norm(h):
        acc = head(h)
        return acc[0:dv] * (1.0 / acc[dv:dv + 1])
    if lam is None:
        per = LANES // dv
        for p in range(n_heads // per):
            blk = jnp.concatenate([norm(p * per + e) for e in range(per)], axis=0)
            o_ref[:, p * LANES:(p + 1) * LANES] = blk.T.astype(o_ref.dtype)
    else:
        for h in range(n_heads // 2):
            o_ref[:, h * dv:(h + 1) * dv] = (norm(2 * h) - lam * norm(2 * h + 1)).T.astype(o_ref.dtype)


def _diff_lambda(lam_ref, lam_init):
    lv = lam_ref[...]
    s1 = jnp.sum(lv[0:1] * lv[1:2], axis=-1, keepdims=True)
    s2 = jnp.sum(lv[2:3] * lv[3:4], axis=-1, keepdims=True)
    return jnp.exp(s1) - jnp.exp(s2) + lam_init


def _flash_kernel(qt_ref, k_ref, vt_ref, sink_ref, lam_ref, o_ref, m_ref, acc_ref,
                  *, n_heads, gk, gv, dv, n_kt, has_sink, lam_init):
    j = pl.program_id(1)
    tq = qt_ref.shape[1]
    dve = dv + SUM_ROWS

    @pl.when(j == 0)
    def _():
        if has_sink:
            ones_row = (lax.broadcasted_iota(jnp.int32, (dve, tq), 0) == dv).astype(F32)
            for h in range(n_heads):
                m_ref[h:h + 1, :] = jnp.full((1, tq), sink_ref[h] * LOG2E, F32)
                acc_ref[h * dve:(h + 1) * dve, :] = ones_row
        else:
            m_ref[...] = jnp.full_like(m_ref, NEG_INF)
            acc_ref[...] = jnp.zeros_like(acc_ref)

    def qk(h):
        pair = (h // gk) // 2
        return jnp.dot(k_ref[:, pair * LANES:(pair + 1) * LANES], qt_ref[h * LANES:(h + 1) * LANES, :],
                       preferred_element_type=F32)

    ahead = 2
    scores = [qk(h) for h in range(ahead)]
    for h in range(n_heads):
        if h + ahead < n_heads:
            scores.append(qk(h + ahead))
        s = scores[h]
        scores[h] = None
        m_old = m_ref[h:h + 1, :]
        m_new = jnp.maximum(m_old, jnp.max(s, axis=0, keepdims=True))
        alpha = jnp.exp2(m_old - m_new)
        p = jnp.exp2(s - m_new).astype(BF16)
        m_ref[h:h + 1, :] = m_new
        vh = h // gv
        pv = jnp.dot(vt_ref[vh * dve:(vh + 1) * dve, :], p, preferred_element_type=F32)
        acc_ref[h * dve:(h + 1) * dve, :] = alpha * acc_ref[h * dve:(h + 1) * dve, :] + pv

    @pl.when(j == n_kt - 1)
    def _():
        lam = None if lam_init is None else _diff_lambda(lam_ref, lam_init)
        _store_heads(o_ref, lambda h: acc_ref[h * dve:(h + 1) * dve, :], n_heads, dv, lam)


def _flash(qt, k, vt, sink, lam_vecs, *, q0, sq, k0, sk, tq, tk, n_heads, gk, gv, dv, has_sink, lam_init):
    nq_rows, nk_cols, nv_rows = qt.shape[0], k.shape[1], vt.shape[0]
    d_out = n_heads * dv if lam_init is None else n_heads * dv // 2
    n_kt = sk // tk
    kern = functools.partial(_flash_kernel, n_heads=n_heads, gk=gk, gv=gv, dv=dv, n_kt=n_kt,
                             has_sink=has_sink, lam_init=lam_init)
    return pl.pallas_call(
        kern,
        grid=(sq // tq, n_kt),
        in_specs=[
            pl.BlockSpec((nq_rows, tq), lambda i, j: (0, q0 // tq + i)),
            pl.BlockSpec((tk, nk_cols), lambda i, j: (k0 // tk + j, 0)),
            pl.BlockSpec((nv_rows, tk), lambda i, j: (0, k0 // tk + j)),
            pl.BlockSpec(memory_space=pltpu.SMEM),
            pl.BlockSpec(lam_vecs.shape, lambda i, j: (0, 0)),
        ],
        out_specs=pl.BlockSpec((tq, d_out), lambda i, j: (i, 0)),
        out_shape=jax.ShapeDtypeStruct((sq, d_out), BF16),
        scratch_shapes=[
            pltpu.VMEM((n_heads, tq), F32),
            pltpu.VMEM((n_heads * (dv + SUM_ROWS), tq), F32),
        ],
        compiler_params=_params(("parallel", "arbitrary")),
        name="flash_attn",
    )(qt, k, vt, sink, lam_vecs)


def _window_kernel(ws_ref, ty_ref, qt_ref, *rest, n_heads, gk, n_win, hb):
    k_refs = rest[0:n_win]
    vt_refs = rest[n_win:2 * n_win]
    kc_ref, vct_ref, bias_ref, sink_ref, o_ref = rest[2 * n_win:]
    tq = qt_ref.shape[1]
    kwin = jnp.concatenate([r[...] for r in k_refs], axis=0)
    vwin = jnp.concatenate([r[...] for r in vt_refs], axis=1)
    dve = HEAD_DIM + SUM_ROWS

    def scores(h):
        pair = (h // gk) // 2
        qm = qt_ref[h * LANES:(h + 1) * LANES, :]
        s_c = jnp.dot(kc_ref[:, pair * LANES:(pair + 1) * LANES], qm, preferred_element_type=F32)
        s_w = jnp.dot(kwin[:, pair * LANES:(pair + 1) * LANES], qm, preferred_element_type=F32)
        return s_c, s_w + bias_ref[0, h if hb > 1 else 0]

    ahead = 2
    sc = [scores(h) for h in range(ahead)]
    accs = []
    for h in range(n_heads):
        if h + ahead < n_heads:
            sc.append(scores(h + ahead))
        s_c, s_w = sc[h]
        sc[h] = None
        g = h // gk
        sink = jnp.full((1, tq), sink_ref[h] * LOG2E, F32)
        m = jnp.maximum(jnp.maximum(jnp.max(s_c, axis=0, keepdims=True), jnp.max(s_w, axis=0, keepdims=True)), sink)
        p_c = jnp.exp2(s_c - m).astype(BF16)
        p_w = jnp.exp2(s_w - m).astype(BF16)
        vrows = slice(g * dve, (g + 1) * dve)
        acc = (jnp.dot(vct_ref[vrows, :], p_c, preferred_element_type=F32)
               + jnp.dot(vwin[vrows, :], p_w, preferred_element_type=F32))
        sum_row = lax.broadcasted_iota(jnp.int32, (dve, tq), 0) == HEAD_DIM
        accs.append(acc + jnp.where(sum_row, jnp.exp2(sink - m), 0.0))
    _store_heads(o_ref, lambda h: accs[h], n_heads, HEAD_DIM, None)


def _windowed(qt, k, vt, bias, sink, wstart, wtype, *, n_lat, n_ctx, n_heads, gk, n_win):
    nq_rows, nk_cols, nv_rows = qt.shape[0], k.shape[1], vt.shape[0]
    hb, wlen = bias.shape[1], bias.shape[2]
    nb = n_lat // QBLK
    kern = functools.partial(_window_kernel, n_heads=n_heads, gk=gk, n_win=n_win, hb=hb)
    k_specs = [pl.BlockSpec((QBLK, nk_cols), functools.partial(lambda b, ws, ty, jj: (ws[b] + jj, 0), jj=jj))
               for jj in range(n_win)]
    v_specs = [pl.BlockSpec((nv_rows, QBLK), functools.partial(lambda b, ws, ty, jj: (0, ws[b] + jj), jj=jj))
               for jj in range(n_win)]
    grid_spec = pltpu.PrefetchScalarGridSpec(
        num_scalar_prefetch=2,
        grid=(nb,),
        in_specs=[pl.BlockSpec((nq_rows, QBLK), lambda b, ws, ty: (0, b))] + k_specs + v_specs + [
            pl.BlockSpec((n_ctx, nk_cols), lambda b, ws, ty: (n_lat // n_ctx, 0)),
            pl.BlockSpec((nv_rows, n_ctx), lambda b, ws, ty: (0, n_lat // n_ctx)),
            pl.BlockSpec((1, hb, wlen, QBLK), lambda b, ws, ty: (ty[b], 0, 0, 0)),
            pl.BlockSpec(memory_space=pltpu.SMEM),
        ],
        out_specs=pl.BlockSpec((QBLK, n_heads * HEAD_DIM), lambda b, ws, ty: (b, 0)),
    )
    return pl.pallas_call(
        kern,
        grid_spec=grid_spec,
        out_shape=jax.ShapeDtypeStruct((n_lat, n_heads * HEAD_DIM), BF16),
        compiler_params=_params(("arbitrary",)),
        name="window_attn",
    )(wstart, wtype, qt, *([k] * n_win), *([vt] * n_win), k, vt, bias, sink)


def _route(logits):
    lane = lax.broadcasted_iota(jnp.int32, logits.shape, 1).astype(F32)
    big = float(LANES)
    lg = jnp.where(lane < N_GROUPS, logits, NEG_INF)
    mg = jnp.max(lg, axis=1, keepdims=True)
    g = jnp.min(jnp.where(lg == mg, lane, big), axis=1, keepdims=True)
    gate = 1.0 / jnp.sum(jnp.exp(lg - mg), axis=1, keepdims=True)
    lo = N_GROUPS + EXPERTS_PER_GROUP * g
    le = jnp.where((lane >= lo) & (lane < lo + EXPERTS_PER_GROUP), logits, NEG_INF)
    v1 = jnp.max(le, axis=1, keepdims=True)
    i1 = jnp.min(jnp.where(le == v1, lane, big), axis=1, keepdims=True)
    le2 = jnp.where(lane == i1, NEG_INF, le)
    v2 = jnp.max(le2, axis=1, keepdims=True)
    i2 = jnp.min(jnp.where(le2 == v2, lane, big), axis=1, keepdims=True)
    e2 = jnp.exp(v2 - v1)
    den = 1.0 + e2
    w1 = gate * (1.0 / den)
    w2 = gate * (e2 / den)
    out = jnp.where(lane == 0, i1 - N_GROUPS, 0.0)
    out = jnp.where(lane == 1, i2 - N_GROUPS, out)
    out = jnp.where(lane == 2, w1, out)
    out = jnp.where(lane == 3, w2, out)
    return out


def _outproj_kernel(ol_ref, oc_ref, wo_ref, x_ref, mod_ref, g2_ref, sub_ref, wr_ref, br_ref,
                    xo_ref, h2_ref, rt_ref, *, n_lat_tiles, d, subln_scale):
    is_ctx = pl.program_id(0) >= n_lat_tiles
    o = jnp.where(is_ctx, oc_ref[...], ol_ref[...])
    if subln_scale is not None:
        parts = []
        for h in range(d // LANES):
            oh = o[:, h * LANES:(h + 1) * LANES].astype(F32)
            yh = oh * lax.rsqrt(jnp.mean(oh * oh, axis=-1, keepdims=True) + NORM_EPS)
            parts.append((yh * sub_ref[...] * subln_scale).astype(BF16))
        o = jnp.concatenate(parts, axis=1)
    y = jnp.dot(o, wo_ref[...], preferred_element_type=F32)
    mod = mod_ref[0]
    row = jnp.where(is_ctx, mod[1:2], mod[0:1])
    x = x_ref[...] + row[:, 2 * d:3 * d] * y
    xo_ref[...] = x
    n = x * lax.rsqrt(jnp.mean(x * x, axis=-1, keepdims=True) + NORM_EPS) * g2_ref[...]
    h2 = n * (1.0 + row[:, 4 * d:5 * d]) + row[:, 3 * d:4 * d]
    h2_ref[...] = h2
    logits = jnp.dot(h2, wr_ref[...], precision=HIGHEST, preferred_element_type=F32) + br_ref[...]
    rt_ref[...] = _route(logits)


def _outproj(o_lat, o_ctx, wo, x, mod, layer, g2, sub, wr, br, *, n_lat, subln_scale):
    t, d = x.shape
    assert o_ctx.shape[0] == ROW_TILE
    nl = n_lat // ROW_TILE
    kern = functools.partial(_outproj_kernel, n_lat_tiles=nl, d=d, subln_scale=subln_scale)
    return pl.pallas_call(
        kern,
        grid=(t // ROW_TILE,),
        in_specs=[
            pl.BlockSpec((ROW_TILE, d), lambda i: (jnp.minimum(i, nl - 1), 0)),
            pl.BlockSpec((ROW_TILE, d), lambda i: (0, 0)),
            pl.BlockSpec((d, d), lambda i: (0, 0)),
            pl.BlockSpec((ROW_TILE, d), lambda i: (i, 0)),
            pl.BlockSpec((1, 8, 6 * d), lambda i: (layer, 0, 0)),
            pl.BlockSpec((1, d), lambda i: (0, 0)),
            pl.BlockSpec((1, LANES), lambda i: (0, 0)),
            pl.BlockSpec((d, LANES), lambda i: (0, 0)),
            pl.BlockSpec((1, LANES), lambda i: (0, 0)),
        ],
        out_specs=[
            pl.BlockSpec((ROW_TILE, d), lambda i: (i, 0)),
            pl.BlockSpec((ROW_TILE, d), lambda i: (i, 0)),
            pl.BlockSpec((ROW_TILE, LANES), lambda i: (i, 0)),
        ],
        out_shape=[
            jax.ShapeDtypeStruct((t, d), F32),
            jax.ShapeDtypeStruct((t, d), F32),
            jax.ShapeDtypeStruct((t, LANES), F32),
        ],
        compiler_params=_params(("parallel",)),
        name="out_proj_router",
    )(o_lat, o_ctx, wo, x, mod, g2, sub, wr, br)


def _gather_rows(idx_ref, base, src_ref, dst_ref, sem, n_rows):
    for r in range(n_rows):
        tok = idx_ref[base + r]
        pltpu.make_async_copy(src_ref.at[pl.ds(tok, 1)], dst_ref.at[pl.ds(r, 1)], sem).start(priority=r % 2)


def _wait_rows(src_ref, dst_ref, sem, n_rows):
    pltpu.make_async_copy(src_ref.at[pl.ds(0, n_rows)], dst_ref, sem).wait()


def _expert_kernel(be_ref, cs_ref, tok_ref, nu_ref, h2_ref, wg_ref, wu_ref, wd_ref, y_ref, xbuf, sems):
    i = pl.program_id(0)
    n_used = nu_ref[0]
    slot = i % 2

    @pl.when(i == 0)
    def _():
        _gather_rows(tok_ref, cs_ref[0], h2_ref, xbuf.at[0], sems.at[0], MOE_TILE)

    @pl.when(i < n_used)
    def _():
        _wait_rows(h2_ref, xbuf.at[slot], sems.at[slot], MOE_TILE)
        x = xbuf[slot].astype(BF16)
        a = jnp.dot(x, wg_ref[...].astype(BF16), preferred_element_type=F32)
        u = jnp.dot(x, wu_ref[...].astype(BF16), preferred_element_type=F32)
        hid = (a * (1.0 / (1.0 + jnp.exp(-a))) * u).astype(BF16)
        y = jnp.dot(hid, wd_ref[...].astype(BF16), preferred_element_type=F32)
        nxt = jnp.minimum(i + 1, n_used - 1)
        _gather_rows(tok_ref, cs_ref[nxt], h2_ref, xbuf.at[1 - slot], sems.at[1 - slot], MOE_TILE)
        y_ref[...] = y

    @pl.when(i >= n_used)
    def _():
        y_ref[...] = jnp.zeros_like(y_ref)

    @pl.when(i == n_used - 1)
    def _():
        _wait_rows(h2_ref, xbuf.at[1 - slot], sems.at[1 - slot], MOE_TILE)


def _experts(blk_e, blk_start, tok_sorted, n_used, h2, wg, wu, wd):
    t, d = h2.shape
    n_blk = blk_e.shape[0]
    de = wg.shape[-1]
    grid_spec = pltpu.PrefetchScalarGridSpec(
        num_scalar_prefetch=4,
        grid=(n_blk,),
        in_specs=[
            pl.BlockSpec(memory_space=pl.ANY),
            pl.BlockSpec((None, d, de), lambda i, be, cs, tk, nu: (be[i], 0, 0)),
            pl.BlockSpec((None, d, de), lambda i, be, cs, tk, nu: (be[i], 0, 0)),
            pl.BlockSpec((None, de, d), lambda i, be, cs, tk, nu: (be[i], 0, 0)),
        ],
        out_specs=pl.BlockSpec((MOE_TILE, d), lambda i, be, cs, tk, nu: (i, 0)),
        scratch_shapes=[pltpu.VMEM((2, MOE_TILE, d), F32), pltpu.SemaphoreType.DMA((2,))],
    )
    return pl.pallas_call(
        _expert_kernel,
        grid_spec=grid_spec,
        out_shape=jax.ShapeDtypeStruct((n_blk * MOE_TILE, d), F32),
        compiler_params=_params(("arbitrary",)),
        name="moe_experts",
    )(blk_e, blk_start, tok_sorted, n_used, h2, wg, wu, wd)


def _combine_kernel(p0_ref, p1_ref, yb_ref, x_ref, rt_ref, mod_ref, xo_ref, abuf, bbuf, sems,
                    *, n_lat_tiles, n_tiles, d):
    i = pl.program_id(0)
    slot = i % 2

    def start(tile, s):
        _gather_rows(p0_ref, tile * ROW_TILE, yb_ref, abuf.at[s], sems.at[0, s], ROW_TILE)
        _gather_rows(p1_ref, tile * ROW_TILE, yb_ref, bbuf.at[s], sems.at[1, s], ROW_TILE)

    @pl.when(i == 0)
    def _():
        start(0, 0)

    _wait_rows(yb_ref, abuf.at[slot], sems.at[0, slot], ROW_TILE)
    _wait_rows(yb_ref, bbuf.at[slot], sems.at[1, slot], ROW_TILE)
    rt = rt_ref[...]
    f = rt[:, 2:3] * abuf[slot] + rt[:, 3:4] * bbuf[slot]
    mod = mod_ref[0]
    row = jnp.where(i >= n_lat_tiles, mod[1:2], mod[0:1])
    out = x_ref[...] + row[:, 5 * d:6 * d] * f

    @pl.when(i + 1 < n_tiles)
    def _():
        start(i + 1, 1 - slot)

    xo_ref[...] = out


def _combine(pos0, pos1, yb, x, route, mod, layer, *, n_lat):
    t, d = x.shape
    n_tiles = t // ROW_TILE
    kern = functools.partial(_combine_kernel, n_lat_tiles=n_lat // ROW_TILE, n_tiles=n_tiles, d=d)
    grid_spec = pltpu.PrefetchScalarGridSpec(
        num_scalar_prefetch=2,
        grid=(n_tiles,),
        in_specs=[
            pl.BlockSpec(memory_space=pl.ANY),
            pl.BlockSpec((ROW_TILE, d), lambda i, p0, p1: (i, 0)),
            pl.BlockSpec((ROW_TILE, LANES), lambda i, p0, p1: (i, 0)),
            pl.BlockSpec((1, 8, 6 * d), lambda i, p0, p1: (layer, 0, 0)),
        ],
        out_specs=pl.BlockSpec((ROW_TILE, d), lambda i, p0, p1: (i, 0)),
        scratch_shapes=[pltpu.VMEM((2, ROW_TILE, d), F32), pltpu.VMEM((2, ROW_TILE, d), F32),
                        pltpu.SemaphoreType.DMA((2, 2))],
    )
    return pl.pallas_call(
        kern,
        grid_spec=grid_spec,
        out_shape=jax.ShapeDtypeStruct((t, d), F32),
        compiler_params=_params(("arbitrary",)),
        name="moe_combine",
    )(pos0, pos1, yb, x, route, mod)


def _final_kernel(x_ref, g_ref, o_ref):
    x = x_ref[...]
    o_ref[...] = x * lax.rsqrt(jnp.mean(x * x, axis=-1, keepdims=True) + NORM_EPS) * g_ref[...]


def _final_norm(x, g, n_lat):
    d = x.shape[1]
    return pl.pallas_call(
        _final_kernel,
        grid=(n_lat // ROW_TILE,),
        in_specs=[pl.BlockSpec((ROW_TILE, d), lambda i: (i, 0)), pl.BlockSpec((1, d), lambda i: (0, 0))],
        out_specs=pl.BlockSpec((ROW_TILE, d), lambda i: (i, 0)),
        out_shape=jax.ShapeDtypeStruct((n_lat, d), F32),
        compiler_params=_params(("parallel",)),
        name="final_norm",
    )(x, g)


def _rope_tables(n_lat, n_ctx):
    t = jnp.arange(n_lat, dtype=jnp.int32)
    pos = jnp.stack([t // GRID_W, t % GRID_W], axis=-1).astype(F32)
    quarter = HEAD_DIM // 4
    inv_freq = 1.0 / (ROPE_THETA ** (jnp.arange(quarter, dtype=F32) / quarter))
    ang = pos[:, :, None] * inv_freq
    cos, sin = jnp.cos(ang), jnp.sin(ang)
    cs = jnp.stack([cos, cos], axis=2).reshape(n_lat, HEAD_DIM)
    sn = jnp.stack([-sin, sin], axis=2).reshape(n_lat, HEAD_DIM)
    cs = jnp.concatenate([cs, jnp.ones((n_ctx, HEAD_DIM), F32)], axis=0)
    sn = jnp.concatenate([sn, jnp.zeros((n_ctx, HEAD_DIM), F32)], axis=0)
    rep = CHUNK // HEAD_DIM
    return jnp.tile(cs, (1, rep)), jnp.tile(sn, (1, rep))


def _swa_tables(n_lat):
    nb = n_lat // QBLK
    n_win = 3
    wstart = np.clip(np.arange(nb) - 1, 0, nb - n_win).astype(np.int32)
    deltas = sorted(set((wstart - np.arange(nb)).tolist()))
    wtype = np.array([deltas.index(d) for d in (wstart - np.arange(nb))], np.int32)
    kj = np.arange(n_win * QBLK)[:, None]
    qi = np.arange(QBLK)[None, :]
    bias = np.stack([np.where(np.abs(dl * QBLK + kj - qi) <= SWA_WINDOW, 0.0, NEG_INF) for dl in deltas])
    return wstart, wtype, bias[:, None].astype(np.float32), n_win


def _na_tables(n_lat):
    rows = n_lat // GRID_W
    kh = min(NA_ROWS, rows)
    rq = QBLK // GRID_W
    nb = n_lat // QBLK
    wrows = kh + rq
    n_win = wrows // rq
    b = np.arange(nb)
    wstart_rows = np.clip(rq * b - kh // 2, 0, rows - wrows)
    wstart = (wstart_rows // rq).astype(np.int32)
    assert np.all(wstart_rows % rq == 0)
    wj = np.arange(wrows)[None, :, None]
    a = np.arange(rq)[None, None, :]
    r = rq * b[:, None, None] + a
    kr = wstart_rows[:, None, None] + wj
    r0 = np.clip(r - kh // 2, 0, rows - kh)
    row_ok = (kr >= r0) & (kr < r0 + kh)
    row_i = np.where(row_ok, kr - r + (NA_ROWS - 1), 0)
    keys = [row_ok[i].tobytes() + row_i[i].tobytes() for i in range(nb)]
    uniq = sorted(set(keys), key=keys.index)
    wtype = np.array([uniq.index(k) for k in keys], np.int32)
    first = [keys.index(k) for k in uniq]
    kc = np.arange(GRID_W)[:, None]
    c = np.arange(GRID_W)[None, :]
    c0 = np.clip(c - NA_COLS // 2, 0, GRID_W - NA_COLS)
    col_ok = (kc >= c0) & (kc < c0 + NA_COLS)
    return wstart, wtype, row_ok[first], row_i[first], col_ok, n_win


def _na_bias(rpb, row_ok, row_i, col_ok):
    n_types, wrows, rq = row_i.shape
    h = rpb.shape[0]
    pad = GRID_W - NA_COLS
    rev = jnp.pad(rpb.astype(F32)[..., ::-1], ((0, 0), (0, 0), (pad, pad)))
    toep = jnp.stack([rev[..., GRID_W - 1 - kc:2 * GRID_W - 1 - kc] for kc in range(GRID_W)], axis=2)
    tiles = jnp.stack([toep[:, i] for i in row_i.reshape(-1).tolist()], axis=1)
    tiles = tiles.reshape(h, n_types, wrows, rq, GRID_W, GRID_W).transpose(1, 0, 2, 4, 3, 5)
    ok = row_ok[:, None, :, None, :, None] & col_ok[None, None, None, :, None, :]
    return jnp.where(ok, tiles * LOG2E, NEG_INF).reshape(n_types, h, wrows * GRID_W, rq * GRID_W)


def _dispatch_tables(route, n_tok):
    expert = route[:, 0:2].astype(jnp.int32).reshape(-1)
    n_assign = n_tok * 2
    n_blk = -(-n_assign // MOE_TILE) + N_EXPERTS
    order = jnp.argsort(expert, stable=True).astype(jnp.int32)
    rank = jnp.argsort(order).astype(jnp.int32)
    onehot = expert[:, None] == jnp.arange(N_EXPERTS, dtype=jnp.int32)[None, :]
    counts = jnp.sum(onehot.astype(jnp.int32), axis=0)
    padded = (counts + MOE_TILE - 1) // MOE_TILE * MOE_TILE
    p_end = jnp.cumsum(padded)
    p_start = p_end - padded
    start = jnp.cumsum(counts) - counts
    pos = rank + jnp.sum(jnp.where(onehot, (p_start - start)[None, :], 0), axis=1)
    n_used = (p_end[-1] // MOE_TILE).astype(jnp.int32)
    blk = jnp.minimum(jnp.arange(n_blk, dtype=jnp.int32), n_used - 1) * MOE_TILE
    blk_e = jnp.minimum(jnp.sum((p_end[None, :] <= blk[:, None]).astype(jnp.int32), axis=1), N_EXPERTS - 1)
    blk_start = start[blk_e] + blk - p_start[blk_e]
    tok_sorted = jnp.concatenate([order // 2, jnp.zeros((MOE_TILE,), jnp.int32)])
    return blk_e, blk_start, tok_sorted, n_used.reshape(1), pos[0::2], pos[1::2]


def kernel(x, c, ctx, c_ctx, mod_w, mod_b, norm1_g, norm2_g, a_w_qkv, a_w_o, a_lam, a_subln_g, b_w_qkv, b_w_o,
           b_sink, c_w_qkv, c_w_o, c_qk_norm_g, d_w_qkv, d_w_o, d_rpb, moe_router_g, moe_router_g_b,
           moe_router_e, moe_router_e_b, moe_w_gate, moe_w_up, moe_w_down, final_g):
    b, n_lat, d = x.shape
    n_ctx = ctx.shape[1]
    depth = mod_w.shape[0]
    assert b == 1 and n_ctx == ROW_TILE and n_lat % (2 * QBLK) == 0 and d % CHUNK == 0
    t = n_lat + n_ctx
    xs = jnp.concatenate([x[0], ctx[0]], axis=0)

    c2 = jnp.zeros((8, d), F32).at[0].set(c[0]).at[1].set(c_ctx)
    mod = _modulation(c2, mod_w, mod_b)

    cs, sn = _rope_tables(n_lat, n_ctx)
    ones_qkg = jnp.ones((2, CHUNK), F32)
    ii = np.arange(CHUNK)
    avg = jnp.asarray((ii[:, None] // HEAD_DIM == ii[None, :] // HEAD_DIM) / HEAD_DIM, BF16)
    no_sink = jnp.full((16,), NEG_INF, F32)
    no_lam = jnp.zeros((4, HEAD_DIM), F32)
    no_sub = jnp.ones((1, LANES), F32)
    n_heads = d // HEAD_DIM

    for i in range(depth):
        m, j = i % 4, i // 4
        g1 = norm1_g[i].reshape(1, d)
        if m == 0:
            nq, nk, nv = d, d, d
            w, wo = a_w_qkv[j], a_w_o[j]
        elif m == 1:
            nq, nk, nv = d, 4 * HEAD_DIM, 4 * HEAD_DIM
            w, wo = b_w_qkv[j], b_w_o[j]
        elif m == 2:
            nq, nk, nv = d, 4 * HEAD_DIM, 4 * HEAD_DIM
            w, wo = c_w_qkv[j], c_w_o[j]
        else:
            nq, nk, nv = d, d, d
            w, wo = d_w_qkv[j], d_w_o[j]
        qkg = jnp.tile(c_qk_norm_g[j], (1, CHUNK // HEAD_DIM)) if m == 2 else ones_qkg
        qt, k, vt = _project(xs, mod, i, g1, w.astype(BF16), cs, sn, qkg, avg, n_lat=n_lat, nq=nq, nk=nk, nv=nv,
                             dv=(2 * HEAD_DIM if m == 0 else HEAD_DIM), gk=(1 if m in (0, 3) else 4), rope=(m != 3),
                             qknorm=(m == 2))

        subln_scale = None
        sub = no_sub
        if m == 0:
            lam_init = 0.8 - 0.6 * math.exp(-0.3 * i)
            kw = dict(n_heads=n_heads, gk=1, gv=2, dv=2 * HEAD_DIM, has_sink=False, lam_init=lam_init)
            o_lat = _flash(qt, k, vt, no_sink, a_lam[j], q0=0, sq=n_lat, k0=0, sk=t, tq=FLASH_TQ, tk=FLASH_TK, **kw)
            o_ctx = _flash(qt, k, vt, no_sink, a_lam[j], q0=n_lat, sq=n_ctx, k0=n_lat, sk=n_ctx, tq=n_ctx,
                           tk=n_ctx, **kw)
            subln_scale = 1.0 - lam_init
            sub = a_subln_g[j].reshape(1, LANES)
        elif m == 2:
            kw = dict(n_heads=n_heads, gk=4, gv=4, dv=HEAD_DIM, has_sink=False, lam_init=None)
            o_lat = _flash(qt, k, vt, no_sink, no_lam, q0=0, sq=n_lat, k0=0, sk=t, tq=FLASH_TQ, tk=FLASH_TK, **kw)
            o_ctx = _flash(qt, k, vt, no_sink, no_lam, q0=n_lat, sq=n_ctx, k0=n_lat, sk=n_ctx, tq=n_ctx,
                           tk=n_ctx, **kw)
        else:
            if m == 1:
                wstart, wtype, bias, n_win = _swa_tables(n_lat)
                bias = jnp.asarray(bias)
                sink, gk = b_sink[j], 4
            else:
                wstart, wtype, row_ok, row_i, col_ok, n_win = _na_tables(n_lat)
                bias = _na_bias(d_rpb[j], row_ok, row_i, col_ok)
                sink, gk = no_sink, 1
            o_lat = _windowed(qt, k, vt, bias, sink, jnp.asarray(wstart), jnp.asarray(wtype), n_lat=n_lat,
                              n_ctx=n_ctx, n_heads=n_heads, gk=gk, n_win=n_win)
            o_ctx = _flash(qt, k, vt, sink, no_lam, q0=n_lat, sq=n_ctx, k0=n_lat, sk=n_ctx, tq=n_ctx, tk=n_ctx,
                           n_heads=n_heads, gk=gk, gv=gk, dv=HEAD_DIM, has_sink=(m == 1), lam_init=None)

        wr = jnp.zeros((d, LANES), F32).at[:, 0:N_GROUPS].set(moe_router_g[i])
        wr = wr.at[:, N_GROUPS:N_GROUPS + N_EXPERTS].set(moe_router_e[i])
        br = jnp.zeros((1, LANES), F32).at[0, 0:N_GROUPS].set(moe_router_g_b[i])
        br = br.at[0, N_GROUPS:N_GROUPS + N_EXPERTS].set(moe_router_e_b[i])
        xs, h2, route = _outproj(o_lat, o_ctx, wo.astype(BF16), xs, mod, i, norm2_g[i].reshape(1, d), sub, wr, br,
                                 n_lat=n_lat, subln_scale=subln_scale)

        blk_e, blk_start, tok_sorted, n_used, pos0, pos1 = _dispatch_tables(route, t)
        yb = _experts(blk_e, blk_start, tok_sorted, n_used, h2, moe_w_gate[i], moe_w_up[i], moe_w_down[i])
        xs = _combine(pos0, pos1, yb, xs, route, mod, i, n_lat=n_lat)

    return _final_norm(xs, final_g.reshape(1, d), n_lat)[None]
```
